```python
import math
import jax, jax.numpy as jnp
from jax import lax
import numpy as np

D_MODEL = 1024
BATCH = 8
SEQ = 2048
DEPTH = 1

N_ATTN_HEADS = 8
ATTN_HEAD_DIM = 64
ATTN_V_DIM = 2 * ATTN_HEAD_DIM
ATTN_QK_WIDTH = N_ATTN_HEADS * 2 * ATTN_HEAD_DIM
ATTN_WIDTH = N_ATTN_HEADS * ATTN_V_DIM
Q_BLOCK = 128
CONV_WIDTH = D_MODEL
CONV_KERNEL = 3
PEER_HEADS = 8
PEER_N_KEYS = 128
PEER_N_EXPERTS = PEER_N_KEYS * PEER_N_KEYS
PEER_KEY_DIM = 256
PEER_HALF_DIM = PEER_KEY_DIM // 2
PEER_TOPK = 16
TOKEN_CHUNK = 128
RMS_EPS = 1e-6

IN_WIDTHS = (ATTN_QK_WIDTH, ATTN_QK_WIDTH, ATTN_WIDTH,
             CONV_WIDTH, CONV_WIDTH, CONV_WIDTH, D_MODEL, D_MODEL)
IN_WIDTH = sum(IN_WIDTHS)
IN_SPLITS = tuple(int(v) for v in np.cumsum(IN_WIDTHS)[:-1])

kernel_name = "hybrid_diffattn_shortconv_peer"


def rmsnorm(x, g):
    xf = x.astype(jnp.float32)
    r = lax.rsqrt(jnp.mean(xf * xf, axis=-1, keepdims=True) + RMS_EPS)
    return (xf * r).astype(x.dtype) * g


def diff_attention(q, k, v, lam):
    seq = q.shape[1]
    scale = ATTN_HEAD_DIM ** -0.5
    outs = []
    for blk in range(seq // Q_BLOCK):
        start = blk * Q_BLOCK
        end = start + Q_BLOCK
        qb, kb, vb = q[:, start:end], k[:, :end], v[:, :end]
        s = jnp.einsum('bqhmd,bkhmd->bhmqk', qb, kb).astype(jnp.float32) * scale
        q_pos = start + jnp.arange(Q_BLOCK)
        k_pos = jnp.arange(end)
        s = jnp.where(k_pos[None, :] <= q_pos[:, None], s, -jnp.inf)
        p = jax.nn.softmax(s, axis=-1)
        a = p[:, :, 0] - lam * p[:, :, 1]
        outs.append(jnp.einsum('bhqk,bkhe->bqhe', a.astype(vb.dtype), vb))
    return jnp.concatenate(outs, axis=1)


def causal_depthwise_conv(u, w):
    return lax.conv_general_dilated(
        u, w[:, None, :].astype(u.dtype), window_strides=(1,),
        padding=[(CONV_KERNEL - 1, 0)],
        dimension_numbers=('NWC', 'WIO', 'NWC'),
        feature_group_count=u.shape[-1])


def peer_ffn(xn, w_query, sub_keys, expert_u, expert_v):
    b, s, d = xn.shape
    t = b * s
    xf = xn.reshape(t, d)
    q = (xf @ w_query).reshape(t, PEER_HEADS, 2, PEER_HALF_DIM)
    half_scores = jnp.einsum('thmd,hmnd->thmn', q, sub_keys)
    top_s, top_i = lax.top_k(half_scores, PEER_TOPK)
    cand_s = (top_s[:, :, 0, :, None] + top_s[:, :, 1, None, :]).reshape(t, PEER_HEADS, PEER_TOPK * PEER_TOPK)
    cand_i = (top_i[:, :, 0, :, None] * PEER_N_KEYS + top_i[:, :, 1, None, :]).reshape(t, PEER_HEADS, PEER_TOPK * PEER_TOPK)
    sel_s, pos = lax.top_k(cand_s, PEER_TOPK)
    sel_i = jnp.take_along_axis(cand_i, pos, axis=-1)
    gates = jax.nn.softmax(sel_s.astype(jnp.float32), axis=-1).astype(xn.dtype)
    hk = PEER_HEADS * PEER_TOPK
    n_chunks = t // TOKEN_CHUNK

    def chunk_fn(args):
        xc, ic, gc = args
        u = expert_u[ic]
        a = jnp.einsum('cd,ced->ce', xc, u)
        hact = jax.nn.gelu(a, approximate=False) * gc
        return jnp.einsum('ce,ced->cd', hact, expert_v[ic])

    y = lax.map(chunk_fn, (xf.reshape(n_chunks, TOKEN_CHUNK, d),
                           sel_i.reshape(n_chunks, TOKEN_CHUNK, hk),
                           gates.reshape(n_chunks, TOKEN_CHUNK, hk)))
    return y.reshape(b, s, d)


def setup_inputs(seed: int = 0) -> dict:
    key = jax.random.key(seed)
    ks = jax.random.split(key, 16)
    f32 = jnp.float32

    def nrm(k, shape, scale):
        return jax.random.normal(k, shape, f32) * scale

    def gain(k, shape):
        return 1.0 + 0.01 * jax.random.normal(k, shape, f32)

    return {
        "x": nrm(ks[0], (BATCH, SEQ, D_MODEL), 1.0),
        "norm1_g": gain(ks[1], (DEPTH, D_MODEL)),
        "w_in": nrm(ks[2], (DEPTH, D_MODEL, IN_WIDTH), D_MODEL ** -0.5),
        "lambda_qk": nrm(ks[3], (DEPTH, 4, ATTN_HEAD_DIM), 0.1),
        "subln_g": gain(ks[4], (DEPTH, ATTN_V_DIM)),
        "conv_w": nrm(ks[5], (DEPTH, CONV_KERNEL, CONV_WIDTH), CONV_KERNEL ** -0.5),
        "w_attn_o": nrm(ks[6], (DEPTH, ATTN_WIDTH, D_MODEL), ATTN_WIDTH ** -0.5),
        "w_conv_o": nrm(ks[7], (DEPTH, CONV_WIDTH, D_MODEL), CONV_WIDTH ** -0.5),
        "w_out": nrm(ks[8], (DEPTH, D_MODEL, D_MODEL), D_MODEL ** -0.5),
        "norm2_g": gain(ks[9], (DEPTH, D_MODEL)),
        "w_query": nrm(ks[10], (DEPTH, D_MODEL, PEER_HEADS * PEER_KEY_DIM), D_MODEL ** -0.5),
        "sub_keys": nrm(ks[11], (DEPTH, PEER_HEADS, 2, PEER_N_KEYS, PEER_HALF_DIM), PEER_HALF_DIM ** -0.5),
        "expert_u": nrm(ks[12], (DEPTH, PEER_N_EXPERTS, D_MODEL), D_MODEL ** -0.5),
        "expert_v": nrm(ks[13], (DEPTH, PEER_N_EXPERTS, D_MODEL), D_MODEL ** -0.5),
        "final_g": gain(ks[14], (D_MODEL,)),
    }


def reference(x, norm1_g, w_in, lambda_qk, subln_g, conv_w, w_attn_o, w_conv_o,
              w_out, norm2_g, w_query, sub_keys, expert_u, expert_v, final_g):
    b, s, _ = x.shape
    h = x
    for l in range(DEPTH):
        lam_init = 0.8 - 0.6 * math.exp(-0.3 * l)
        lq = lambda_qk[l].astype(jnp.float32)
        lam = jnp.exp(jnp.sum(lq[0] * lq[1])) - jnp.exp(jnp.sum(lq[2] * lq[3])) + lam_init

        n = rmsnorm(h, norm1_g[l])
        proj = n @ w_in[l]
        q, k, v, cb, cc, cx, g_attn, g_conv = jnp.split(proj, IN_SPLITS, axis=-1)

        q = q.reshape(b, s, N_ATTN_HEADS, 2, ATTN_HEAD_DIM)
        k = k.reshape(b, s, N_ATTN_HEADS, 2, ATTN_HEAD_DIM)
        v = v.reshape(b, s, N_ATTN_HEADS, ATTN_V_DIM)
        o = diff_attention(q, k, v, lam)
        o = rmsnorm(o, subln_g[l]) * (1.0 - lam_init)
        y_attn = o.reshape(b, s, ATTN_WIDTH) @ w_attn_o[l]

        y_conv = (cb * causal_depthwise_conv(cc * cx, conv_w[l])) @ w_conv_o[l]

        merged = jax.nn.sigmoid(g_attn) * y_attn + jax.nn.sigmoid(g_conv) * y_conv
        h = h + merged @ w_out[l]

        h = h + peer_ffn(rmsnorm(h, norm2_g[l]), w_query[l], sub_keys[l], expert_u[l], expert_v[l])
    return rmsnorm(h, final_g)
```

```python
import functools
import math

import jax
import jax.numpy as jnp
from jax import lax
from jax.experimental import pallas as pl
from jax.experimental.pallas import tpu as pltpu

F32 = jnp.float32
BF16 = jnp.bfloat16
I32 = jnp.int32

RMS_EPS = 1e-6
N_HEADS = 8
HEAD_DIM = 64
V_DIM = 128
PEER_HEADS = 8
N_KEYS = 128
HALF_DIM = 128
TOPK = 16
LANES = 128
SUBLANES = 8
VMEM_LIMIT = 56 * 1024 * 1024

NT_DIMS = (((1,), (1,)), ((), ()))


def _gelu(x):
    return 0.5 * x * (1.0 + lax.erf(x * (2.0 ** -0.5)))


def _rms_scale(x):
    return lax.rsqrt(jnp.mean(x * x, axis=-1, keepdims=True) + RMS_EPS)


def _inproj_kernel(x_ref, g1_ref, w_ref, cw_ref, qkv_ref, z_ref, sa_ref, sc_ref, n_ref, carry_ref,
                   *, tiles_per_seq, d, cb):
    i = pl.program_id(0)
    tt = x_ref.shape[0]
    x = x_ref[...]
    n_ref[...] = ((x * _rms_scale(x)) * g1_ref[...]).astype(BF16)

    @pl.when(i % tiles_per_seq == 0)
    def _():
        carry_ref[...] = jnp.zeros_like(carry_ref)

    def proj(col):
        return jnp.dot(n_ref[...], w_ref[:, col:col + cb], preferred_element_type=F32)

    for j in range(3 * d // cb):
        qkv_ref[:, j * cb:(j + 1) * cb] = proj(j * cb).astype(BF16)
    for j in range(d // cb):
        c0 = j * cb
        gate_b = proj(3 * d + c0)
        u = proj(4 * d + c0) * proj(5 * d + c0)
        ext = jnp.concatenate([carry_ref[:, c0:c0 + cb], u], axis=0)
        conv = (cw_ref[0:1, c0:c0 + cb] * ext[SUBLANES - 2:SUBLANES - 2 + tt]
                + cw_ref[1:2, c0:c0 + cb] * ext[SUBLANES - 1:SUBLANES - 1 + tt]
                + cw_ref[2:3, c0:c0 + cb] * u)
        z_ref[:, c0:c0 + cb] = (gate_b * conv).astype(BF16)
        carry_ref[:, c0:c0 + cb] = u[tt - SUBLANES:, :]
    for j in range(d // cb):
        c0 = j * cb
        sa_ref[:, c0:c0 + cb] = jax.nn.sigmoid(proj(6 * d + c0))
        sc_ref[:, c0:c0 + cb] = jax.nn.sigmoid(proj(7 * d + c0))


def _in_proj(x2, g1, w_in, conv_w, seq, tt=256, cb=512):
    t, d = x2.shape
    kern = functools.partial(_inproj_kernel, tiles_per_seq=seq // tt, d=d, cb=cb)
    row = lambda i: (i, 0)
    fixed = lambda i: (0, 0)
    return pl.pallas_call(
        kern,
        grid=(t // tt,),
        in_specs=[pl.BlockSpec((tt, d), row),
                  pl.BlockSpec((1, d), fixed),
                  pl.BlockSpec((d, 8 * d), fixed, pipeline_mode=pl.Buffered(1)),
                  pl.BlockSpec((3, d), fixed)],
        out_specs=[pl.BlockSpec((tt, 3 * d), row),
                   pl.BlockSpec((tt, d), row),
                   pl.BlockSpec((tt, d), row),
                   pl.BlockSpec((tt, d), row)],
        out_shape=[jax.ShapeDtypeStruct((t, 3 * d), BF16),
                   jax.ShapeDtypeStruct((t, d), BF16),
                   jax.ShapeDtypeStruct((t, d), F32),
                   jax.ShapeDtypeStruct((t, d), F32)],
        scratch_shapes=[pltpu.VMEM((tt, d), BF16), pltpu.VMEM((SUBLANES, d), F32)],
        compiler_params=pltpu.CompilerParams(dimension_semantics=("arbitrary",),
                                             vmem_limit_bytes=VMEM_LIMIT),
        name="in_proj",
    )(x2, g1, w_in, conv_w)


def _attn_kernel(q_ref, k_ref, v_ref, lam_ref, sg_ref, o_ref, *, lam_init, bq):
    i = pl.program_id(2)
    lq = lam_ref[...]
    lam = (jnp.exp(jnp.sum(lq[0:1] * lq[1:2], axis=-1, keepdims=True))
           - jnp.exp(jnp.sum(lq[2:3] * lq[3:4], axis=-1, keepdims=True)) + lam_init)

    q = q_ref[...] * (HEAD_DIM ** -0.5)
    lane = lax.broadcasted_iota(I32, q.shape, 1)
    zero = jnp.zeros_like(q)
    qs = (jnp.where(lane < HEAD_DIM, q, zero), jnp.where(lane >= HEAD_DIM, q, zero))

    def chunk(c, state, masked):
        off = pl.multiple_of(c * bq, bq)
        kc = k_ref[pl.ds(off, bq), :]
        vc = v_ref[pl.ds(off, bq), :]
        new = []
        for mp in range(2):
            m_old, l_old, acc = state[3 * mp:3 * mp + 3]
            s = lax.dot_general(qs[mp], kc, NT_DIMS, preferred_element_type=F32)
            if masked:
                r = lax.broadcasted_iota(I32, s.shape, 0)
                cc = lax.broadcasted_iota(I32, s.shape, 1)
                s = jnp.where(cc <= r, s, -jnp.inf)
            m_new = jnp.maximum(m_old, jnp.max(s, axis=-1, keepdims=True))
            alpha = jnp.exp(m_old - m_new)
            p = jnp.exp(s - m_new)
            l_new = alpha * l_old + jnp.sum(p, axis=-1, keepdims=True)
            acc = alpha * acc + jnp.dot(p.astype(BF16), vc, preferred_element_type=F32)
            new += [m_new, l_new, acc]
        return tuple(new)

    init = []
    for _ in range(2):
        init += [jnp.full((bq, 1), -jnp.inf, F32), jnp.zeros((bq, 1), F32), jnp.zeros((bq, V_DIM), F32)]
    state = lax.fori_loop(0, i, lambda c, st: chunk(c, st, False), tuple(init))
    m0, l0, acc0, m1, l1, acc1 = chunk(i, state, True)

    o = acc0 / l0 - lam * (acc1 / l1)
    o = (o * _rms_scale(o)) * sg_ref[...] * (1.0 - lam_init)
    o_ref[...] = o.astype(BF16)


def _diff_attn(qkv, lambda_qk, subln_g, batch, seq, lam_init, bq=256):
    t = qkv.shape[0]
    nq = seq // bq
    kern = functools.partial(_attn_kernel, lam_init=lam_init, bq=bq)
    return pl.pallas_call(
        kern,
        grid=(batch, N_HEADS, nq),
        in_specs=[pl.BlockSpec((bq, V_DIM), lambda b, h, i: (b * nq + i, h)),
                  pl.BlockSpec((seq, V_DIM), lambda b, h, i: (b, N_HEADS + h)),
                  pl.BlockSpec((seq, V_DIM), lambda b, h, i: (b, 2 * N_HEADS + h)),
                  pl.BlockSpec((4, HEAD_DIM), lambda b, h, i: (0, 0)),
                  pl.BlockSpec((1, V_DIM), lambda b, h, i: (0, 0))],
        out_specs=pl.BlockSpec((bq, V_DIM), lambda b, h, i: (b * nq + i, h)),
        out_shape=jax.ShapeDtypeStruct((t, N_HEADS * V_DIM), BF16),
        compiler_params=pltpu.CompilerParams(dimension_semantics=("arbitrary",) * 3,
                                             vmem_limit_bytes=VMEM_LIMIT),
        name="diff_attn",
    )(qkv, qkv, qkv, lambda_qk, subln_g)


def _mixout_kernel(o_ref, z_ref, sa_ref, sc_ref, x_ref, wao_ref, wco_ref, wout_ref, g2_ref, wq_ref,
                   h1_ref, xn_ref, q2_ref):
    ya = jnp.dot(o_ref[...], wao_ref[...], preferred_element_type=F32)
    yc = jnp.dot(z_ref[...], wco_ref[...], preferred_element_type=F32)
    merged = sa_ref[...] * ya + sc_ref[...] * yc
    h1 = x_ref[...] + jnp.dot(merged.astype(BF16), wout_ref[...], preferred_element_type=F32)
    h1_ref[...] = h1
    xn = ((h1 * _rms_scale(h1)) * g2_ref[...]).astype(BF16)
    xn_ref[...] = xn
    q2 = jnp.dot(xn, wq_ref[...], preferred_element_type=F32).astype(BF16)
    for j in range(q2_ref.shape[0]):
        q2_ref[j] = q2[:, j * HALF_DIM:(j + 1) * HALF_DIM]


def _mix_out(o, z, sa, sc, x2, wao, wco, wout, g2, wq, tt=256):
    t, d = x2.shape
    nq = wq.shape[1] // HALF_DIM
    row = lambda i: (i, 0)
    fixed = lambda i: (0, 0)
    return pl.pallas_call(
        _mixout_kernel,
        grid=(t // tt,),
        in_specs=[pl.BlockSpec((tt, d), row), pl.BlockSpec((tt, d), row),
                  pl.BlockSpec((tt, d), row), pl.BlockSpec((tt, d), row),
                  pl.BlockSpec((tt, d), row),
                  pl.BlockSpec((d, d), fixed), pl.BlockSpec((d, d), fixed), pl.BlockSpec((d, d), fixed),
                  pl.BlockSpec((1, d), fixed),
                  pl.BlockSpec((d, nq * HALF_DIM), fixed)],
        out_specs=[pl.BlockSpec((tt, d), row), pl.BlockSpec((tt, d), row),
                   pl.BlockSpec((nq, tt, HALF_DIM), lambda i: (0, i, 0))],
        out_shape=[jax.ShapeDtypeStruct((t, d), F32),
                   jax.ShapeDtypeStruct((t, d), BF16),
                   jax.ShapeDtypeStruct((nq, t, HALF_DIM), BF16)],
        compiler_params=pltpu.CompilerParams(dimension_semantics=("arbitrary",),
                                             vmem_limit_bytes=VMEM_LIMIT),
        name="mix_out",
    )(o, z, sa, sc, x2, wao, wco, wout, g2, wq)


def _cand_layout():
    r = lax.broadcasted_iota(I32, (80, LANES), 0)
    mid = ((r - 8) >> 3) * TOPK + (r & 7)
    flat = jnp.where(r < 16, r, jnp.where(r < 72, mid, (r - 64) * TOPK))
    valid = ((flat >> 4) + 1) * ((flat & 15) + 1) <= TOPK
    return flat, valid


def _route_kernel(q2_ref, keys_ref, a_ref, b_ref, g_ref, v_s, i_s, ss_s, as_s, bs_s, at_s, bt_s, gt_s, *, n_lg):
    big = jnp.int32(1 << 20)
    neg = jnp.float32(-jnp.inf)
    rows_k = lax.broadcasted_iota(I32, (N_KEYS, LANES), 0)
    rows_t = lax.broadcasted_iota(I32, (TOPK, LANES), 0)
    flat, valid = _cand_layout()

    def body(it, carry):
        h = it // n_lg
        tok = pl.multiple_of((it % n_lg) * LANES, LANES)
        for mp in range(2):
            qt = q2_ref[2 * h + mp, pl.ds(tok, LANES), :]
            s = lax.dot_general(keys_ref[h, mp], qt, NT_DIMS, preferred_element_type=F32)
            for k in range(TOPK):
                m = jnp.max(s, axis=0, keepdims=True)
                idx = jnp.min(jnp.where(s == m, rows_k, big), axis=0, keepdims=True)
                v_s[mp, k:k + 1, :] = m
                i_s[mp, k:k + 1, :] = idx
                s = jnp.where(rows_k == idx, neg, s)
        pieces = [v_s[0, 0:1, :] + v_s[1, :, :]]
        for ka in range(1, 8):
            pieces.append(v_s[0, ka:ka + 1, :] + v_s[1, 0:8, :])
        pieces.append(v_s[0, 8:16, :] + v_s[1, 0:1, :])
        cand = jnp.where(valid, jnp.concatenate(pieces, axis=0), neg)
        idx_a = i_s[0]
        idx_b = i_s[1]
        for k in range(TOPK):
            m = jnp.max(cand, axis=0, keepdims=True)
            p = jnp.min(jnp.where(cand == m, flat, big), axis=0, keepdims=True)
            cand = jnp.where(flat == p, neg, cand)
            ss_s[k:k + 1, :] = m
            as_s[k:k + 1, :] = jnp.max(jnp.where(rows_t == (p >> 4), idx_a, -1), axis=0, keepdims=True)
            bs_s[k:k + 1, :] = jnp.max(jnp.where(rows_t == (p & 15), idx_b, -1), axis=0, keepdims=True)
        ss = ss_s[...]
        e = jnp.exp(ss - ss[0:1])
        gates = e / jnp.sum(e, axis=0, keepdims=True)
        row0 = pl.multiple_of(h * TOPK, TOPK)
        at_s[pl.ds(row0, TOPK), pl.ds(tok, LANES)] = as_s[...]
        bt_s[pl.ds(row0, TOPK), pl.ds(tok, LANES)] = bs_s[...]
        gt_s[pl.ds(row0, TOPK), pl.ds(tok, LANES)] = gates
        return carry

    lax.fori_loop(0, PEER_HEADS * n_lg, body, 0)
    for lg in range(n_lg):
        sl = slice(lg * LANES, (lg + 1) * LANES)
        a_ref[sl, :] = at_s[:, sl].T
        b_ref[sl, :] = bt_s[:, sl].T
        g_ref[sl, :] = gt_s[:, sl].T


def _route(q2, keys, tt=512):
    nq, t, _ = q2.shape
    n_lg = tt // LANES
    hk = PEER_HEADS * TOPK
    kern = functools.partial(_route_kernel, n_lg=n_lg)
    row = lambda i: (i, 0)
    return pl.pallas_call(
        kern,
        grid=(t // tt,),
        in_specs=[pl.BlockSpec((nq, tt, HALF_DIM), lambda i: (0, i, 0)),
                  pl.BlockSpec((PEER_HEADS, 2, N_KEYS, HALF_DIM), lambda i: (0, 0, 0, 0))],
        out_specs=[pl.BlockSpec((tt, hk), row)] * 3,
        out_shape=[jax.ShapeDtypeStruct((t, hk), I32),
                   jax.ShapeDtypeStruct((t, hk), I32),
                   jax.ShapeDtypeStruct((t, hk), F32)],
        scratch_shapes=[pltpu.VMEM((2, TOPK, LANES), F32), pltpu.VMEM((2, TOPK, LANES), I32),
                        pltpu.VMEM((TOPK, LANES), F32), pltpu.VMEM((TOPK, LANES), I32),
                        pltpu.VMEM((TOPK, LANES), I32),
                        pltpu.VMEM((hk, tt), I32), pltpu.VMEM((hk, tt), I32), pltpu.VMEM((hk, tt), F32)],
        compiler_params=pltpu.CompilerParams(dimension_semantics=("arbitrary",),
                                             vmem_limit_bytes=VMEM_LIMIT),
        name="route",
    )(q2, keys)


def _peer_kernel(a_ref, b_ref, g_ref, xn_ref, u_ref, v_ref, h1_ref, fg_ref, out_ref, gsc, yacc,
                 *, pitch, ec, sub):
    c = pl.program_id(1)
    tt = xn_ref.shape[0]

    @pl.when(c == 0)
    def _():
        yacc[...] = jnp.zeros_like(yacc)
        rows = lax.broadcasted_iota(I32, (N_KEYS, LANES), 0)

        def tok(t, carry):
            pt = jnp.where(rows == a_ref[pl.ds(t, 1), :], g_ref[pl.ds(t, 1), :], 0.0).astype(BF16)
            qt = jnp.where(rows == b_ref[pl.ds(t, 1), :], 1.0, 0.0).astype(BF16)
            gsc[pl.ds(t, N_KEYS, stride=pitch), :] = lax.dot_general(pt, qt, NT_DIMS, preferred_element_type=F32)
            return carry

        lax.fori_loop(0, tt, tok, 0)

    xn = xn_ref[...]
    planes = sub // N_KEYS
    for s in range(ec // sub):
        a = lax.dot_general(xn, u_ref[s * sub:(s + 1) * sub, :], NT_DIMS, preferred_element_type=F32)
        i0 = c * (ec // N_KEYS) + s * planes
        gate = jnp.concatenate(
            [gsc[pl.ds(pl.multiple_of((i0 + k) * pitch, SUBLANES), tt), :] for k in range(planes)], axis=1)
        hact = (_gelu(a) * gate).astype(BF16)
        yacc[...] += jnp.dot(hact, v_ref[s * sub:(s + 1) * sub, :], preferred_element_type=F32)

    @pl.when(c == pl.num_programs(1) - 1)
    def _():
        h2 = h1_ref[...] + yacc[...]
        out_ref[...] = (h2 * _rms_scale(h2)) * fg_ref[...]


def _peer(a, b, g, xn, u, v, h1, fg, tt=256, ec=1024, sub=256):
    t, d = xn.shape
    ne = u.shape[0]
    pitch = tt + SUBLANES
    kern = functools.partial(_peer_kernel, pitch=pitch, ec=ec, sub=sub)
    row = lambda i, c: (i, 0)
    return pl.pallas_call(
        kern,
        grid=(t // tt, ne // ec),
        in_specs=[pl.BlockSpec((tt, LANES), row), pl.BlockSpec((tt, LANES), row), pl.BlockSpec((tt, LANES), row),
                  pl.BlockSpec((tt, d), row),
                  pl.BlockSpec((ec, d), lambda i, c: (c, 0)),
                  pl.BlockSpec((ec, d), lambda i, c: (c, 0)),
                  pl.BlockSpec((tt, d), row),
                  pl.BlockSpec((1, d), lambda i, c: (0, 0))],
        out_specs=pl.BlockSpec((tt, d), row),
        out_shape=jax.ShapeDtypeStruct((t, d), F32),
        scratch_shapes=[pltpu.VMEM((N_KEYS * pitch, LANES), F32), pltpu.VMEM((tt, d), F32)],
        compiler_params=pltpu.CompilerParams(dimension_semantics=("arbitrary", "arbitrary"),
                                             vmem_limit_bytes=VMEM_LIMIT),
        name="peer",
    )(a, b, g, xn, u, v, h1, fg)


def kernel(x, norm1_g, w_in, lambda_qk, subln_g, conv_w, w_attn_o, w_conv_o, w_out, norm2_g, w_query,
           sub_keys, expert_u, expert_v, final_g):
    batch, seq, d = x.shape
    assert w_in.shape[0] == 1, "the final RMSNorm is fused into the single layer's PEER kernel"
    l = 0
    h = x.reshape(batch * seq, d)
    lam_init = 0.8 - 0.6 * math.exp(-0.3 * l)
    qkv, z, sa, sc = _in_proj(h, norm1_g[l][None], w_in[l].astype(BF16), conv_w[l], seq)
    o = _diff_attn(qkv, lambda_qk[l], subln_g[l][None], batch, seq, lam_init)
    h1, xn, q2 = _mix_out(o, z, sa, sc, h, w_attn_o[l].astype(BF16), w_conv_o[l].astype(BF16),
                          w_out[l].astype(BF16), norm2_g[l][None], w_query[l].astype(BF16))
    a, b, g = _route(q2, sub_keys[l].astype(BF16))
    out = _peer(a, b, g, xn, expert_u[l].astype(BF16), expert_v[l].astype(BF16), h1, final_g[None])
    return out.reshape(batch, seq, d)
```

```python
import functools
import math

import jax
import jax.numpy as jnp
from jax import lax
from jax.experimental import pallas as pl
from jax.experimental.pallas import tpu as pltpu

F32 = jnp.float32
BF16 = jnp.bfloat16
I32 = jnp.int32

RMS_EPS = 1e-6
N_HEADS = 8
HEAD_DIM = 64
V_DIM = 128
PEER_HEADS = 8
N_KEYS = 128
HALF_DIM = 128
TOPK = 16
LANES = 128
SUBLANES = 8
VMEM_LIMIT = 56 * 1024 * 1024

NT_DIMS = (((1,), (1,)), ((), ()))


def _gelu(x):
    return 0.5 * x * (1.0 + lax.erf(x * (2.0 ** -0.5)))


def _rms_scale(x):
    return lax.rsqrt(jnp.mean(x * x, axis=-1, keepdims=True) + RMS_EPS)


def _inproj_kernel(x_ref, g1_ref, w_ref, cw_ref, qkv_ref, z_ref, sa_ref, sc_ref, n_ref, carry_ref,
                   *, tiles_per_seq, d, cb):
    i = pl.program_id(0)
    tt = x_ref.shape[0]
    x = x_ref[...]
    n_ref[...] = ((x * _rms_scale(x)) * g1_ref[...]).astype(BF16)

    @pl.when(i % tiles_per_seq == 0)
    def _():
        carry_ref[...] = jnp.zeros_like(carry_ref)

    def proj(col):
        return jnp.dot(n_ref[...], w_ref[:, col:col + cb], preferred_element_type=F32)

    for j in range(3 * d // cb):
        qkv_ref[:, j * cb:(j + 1) * cb] = proj(j * cb).astype(BF16)
    for j in range(d // cb):
        c0 = j * cb
        gate_b = proj(3 * d + c0)
        u = proj(4 * d + c0) * proj(5 * d + c0)
        ext = jnp.concatenate([carry_ref[:, c0:c0 + cb], u], axis=0)
        conv = (cw_ref[0:1, c0:c0 + cb] * ext[SUBLANES - 2:SUBLANES - 2 + tt]
                + cw_ref[1:2, c0:c0 + cb] * ext[SUBLANES - 1:SUBLANES - 1 + tt]
                + cw_ref[2:3, c0:c0 + cb] * u)
        z_ref[:, c0:c0 + cb] = (gate_b * conv).astype(BF16)
        carry_ref[:, c0:c0 + cb] = u[tt - SUBLANES:, :]
    for j in range(d // cb):
        c0 = j * cb
        sa_ref[:, c0:c0 + cb] = jax.nn.sigmoid(proj(6 * d + c0))
        sc_ref[:, c0:c0 + cb] = jax.nn.sigmoid(proj(7 * d + c0))


def _in_proj(x2, g1, w_in, conv_w, seq, tt=256, cb=512):
    t, d = x2.shape
    kern = functools.partial(_inproj_kernel, tiles_per_seq=seq // tt, d=d, cb=cb)
    row = lambda i: (i, 0)
    fixed = lambda i: (0, 0)
    return pl.pallas_call(
        kern,
        grid=(t // tt,),
        in_specs=[pl.BlockSpec((tt, d), row),
                  pl.BlockSpec((1, d), fixed),
                  pl.BlockSpec((d, 8 * d), fixed, pipeline_mode=pl.Buffered(1)),
                  pl.BlockSpec((3, d), fixed)],
        out_specs=[pl.BlockSpec((tt, 3 * d), row),
                   pl.BlockSpec((tt, d), row),
                   pl.BlockSpec((tt, d), row),
                   pl.BlockSpec((tt, d), row)],
        out_shape=[jax.ShapeDtypeStruct((t, 3 * d), BF16),
                   jax.ShapeDtypeStruct((t, d), BF16),
                   jax.ShapeDtypeStruct((t, d), F32),
                   jax.ShapeDtypeStruct((t, d), F32)],
        scratch_shapes=[pltpu.VMEM((tt, d), BF16), pltpu.VMEM((SUBLANES, d), F32)],
        compiler_params=pltpu.CompilerParams(dimension_semantics=("arbitrary",),
                                             vmem_limit_bytes=VMEM_LIMIT),
        name="in_proj",
    )(x2, g1, w_in, conv_w)


def _attn_kernel(q_ref, k_ref, v_ref, lam_ref, sg_ref, o_ref, s_sc, mrun_sc, lrun_sc, acc_sc,
                 *, lam_init, bq, hps):
    i = pl.program_id(2)
    lq = lam_ref[...]
    lam = (jnp.exp(jnp.sum(lq[0:1] * lq[1:2], axis=-1, keepdims=True))
           - jnp.exp(jnp.sum(lq[2:3] * lq[3:4], axis=-1, keepdims=True)) + lam_init)
    nt = bq // LANES
    heads = [slice(hh * V_DIM, (hh + 1) * V_DIM) for hh in range(hps)]

    q2 = []
    for cols in heads:
        q = q_ref[:, cols] * (HEAD_DIM ** -0.5)
        lane = lax.broadcasted_iota(I32, q.shape, 1)
        zero = jnp.zeros_like(q)
        q2.append(jnp.concatenate([jnp.where(lane < HEAD_DIM, q, zero), jnp.where(lane >= HEAD_DIM, q, zero)],
                                  axis=0))

    def scores(hh, c):
        off = pl.multiple_of(c * bq, bq)
        return lax.dot_general(q2[hh], k_ref[pl.ds(off, bq), heads[hh]], NT_DIMS,
                               preferred_element_type=F32)

    def tile_max(s):
        return functools.reduce(jnp.maximum, [s[:, j * LANES:(j + 1) * LANES] for j in range(nt)])

    mrun_sc[...] = jnp.full(mrun_sc.shape, -jnp.inf, F32)

    def pass1(c, carry):
        for hh in range(hps):
            s = scores(hh, c)
            s_sc[hh, c] = s
            mrun_sc[hh] = jnp.maximum(mrun_sc[hh], tile_max(s))
        return carry

    lax.fori_loop(0, i, pass1, 0)
    for hh in range(hps):
        s = scores(hh, i)
        r = lax.broadcasted_iota(I32, s.shape, 0) & (bq - 1)
        cc = lax.broadcasted_iota(I32, s.shape, 1)
        s = jnp.where(cc <= r, s, -jnp.inf)
        s_sc[hh, i] = s
        mrow = jnp.max(jnp.maximum(mrun_sc[hh], tile_max(s)), axis=-1, keepdims=True)
        mrun_sc[hh] = jnp.broadcast_to(mrow, mrun_sc.shape[1:])

    lrun_sc[...] = jnp.zeros_like(lrun_sc)
    acc_sc[...] = jnp.zeros_like(acc_sc)

    def pass2(c, carry):
        off = pl.multiple_of(c * bq, bq)
        for hh in range(hps):
            mb = mrun_sc[hh]
            s = s_sc[hh, c]
            ps = [jnp.exp(s[:, j * LANES:(j + 1) * LANES] - mb) for j in range(nt)]
            lrun_sc[hh] += functools.reduce(jnp.add, ps)
            p = jnp.concatenate(ps, axis=1).astype(BF16)
            acc_sc[hh] += jnp.dot(p, v_ref[pl.ds(off, bq), heads[hh]], preferred_element_type=F32)
        return carry

    lax.fori_loop(0, i + 1, pass2, 0)
    for hh in range(hps):
        den = jnp.sum(lrun_sc[hh], axis=-1, keepdims=True)
        on = acc_sc[hh] / den
        o = on[:bq] - lam * on[bq:]
        o = (o * _rms_scale(o)) * sg_ref[...] * (1.0 - lam_init)
        o_ref[:, heads[hh]] = o.astype(BF16)


def _diff_attn(qkv, lambda_qk, subln_g, batch, seq, lam_init, bq=256, hps=4):
    t = qkv.shape[0]
    nq = seq // bq
    hb = N_HEADS // hps
    wb = hps * V_DIM
    assert bq & (bq - 1) == 0 and bq % LANES == 0
    kern = functools.partial(_attn_kernel, lam_init=lam_init, bq=bq, hps=hps)
    return pl.pallas_call(
        kern,
        grid=(batch, hb, nq),
        in_specs=[pl.BlockSpec((bq, wb), lambda b, h, i: (b * nq + i, h)),
                  pl.BlockSpec((seq, wb), lambda b, h, i: (b, hb + h)),
                  pl.BlockSpec((seq, wb), lambda b, h, i: (b, 2 * hb + h)),
                  pl.BlockSpec((4, HEAD_DIM), lambda b, h, i: (0, 0)),
                  pl.BlockSpec((1, V_DIM), lambda b, h, i: (0, 0))],
        out_specs=pl.BlockSpec((bq, wb), lambda b, h, i: (b * nq + i, h)),
        out_shape=jax.ShapeDtypeStruct((t, N_HEADS * V_DIM), BF16),
        scratch_shapes=[pltpu.VMEM((hps, nq, 2 * bq, bq), F32), pltpu.VMEM((hps, 2 * bq, LANES), F32),
                        pltpu.VMEM((hps, 2 * bq, LANES), F32), pltpu.VMEM((hps, 2 * bq, V_DIM), F32)],
        compiler_params=pltpu.CompilerParams(dimension_semantics=("arbitrary",) * 3,
                                             vmem_limit_bytes=VMEM_LIMIT),
        name="diff_attn",
    )(qkv, qkv, qkv, lambda_qk, subln_g)


def _mixout_kernel(o_ref, z_ref, sa_ref, sc_ref, x_ref, wao_ref, wco_ref, wout_ref, g2_ref, wq_ref,
                   h1_ref, xn_ref, q2_ref):
    ya = jnp.dot(o_ref[...], wao_ref[...], preferred_element_type=F32)
    yc = jnp.dot(z_ref[...], wco_ref[...], preferred_element_type=F32)
    merged = sa_ref[...] * ya + sc_ref[...] * yc
    h1 = x_ref[...] + jnp.dot(merged.astype(BF16), wout_ref[...], preferred_element_type=F32)
    h1_ref[...] = h1
    xn = ((h1 * _rms_scale(h1)) * g2_ref[...]).astype(BF16)
    xn_ref[...] = xn
    q2 = jnp.dot(xn, wq_ref[...], preferred_element_type=F32).astype(BF16)
    for j in range(q2_ref.shape[0]):
        q2_ref[j] = q2[:, j * HALF_DIM:(j + 1) * HALF_DIM]


def _mix_out(o, z, sa, sc, x2, wao, wco, wout, g2, wq, tt=256):
    t, d = x2.shape
    nq = wq.shape[1] // HALF_DIM
    row = lambda i: (i, 0)
    fixed = lambda i: (0, 0)
    return pl.pallas_call(
        _mixout_kernel,
        grid=(t // tt,),
        in_specs=[pl.BlockSpec((tt, d), row), pl.BlockSpec((tt, d), row),
                  pl.BlockSpec((tt, d), row), pl.BlockSpec((tt, d), row),
                  pl.BlockSpec((tt, d), row),
                  pl.BlockSpec((d, d), fixed), pl.BlockSpec((d, d), fixed), pl.BlockSpec((d, d), fixed),
                  pl.BlockSpec((1, d), fixed),
                  pl.BlockSpec((d, nq * HALF_DIM), fixed)],
        out_specs=[pl.BlockSpec((tt, d), row), pl.BlockSpec((tt, d), row),
                   pl.BlockSpec((nq, tt, HALF_DIM), lambda i: (0, i, 0))],
        out_shape=[jax.ShapeDtypeStruct((t, d), F32),
                   jax.ShapeDtypeStruct((t, d), BF16),
                   jax.ShapeDtypeStruct((nq, t, HALF_DIM), BF16)],
        compiler_params=pltpu.CompilerParams(dimension_semantics=("arbitrary",),
                                             vmem_limit_bytes=VMEM_LIMIT),
        name="mix_out",
    )(o, z, sa, sc, x2, wao, wco, wout, g2, wq)


def _cand_layout():
    r = lax.broadcasted_iota(I32, (80, LANES), 0)
    mid = ((r - 8) >> 3) * TOPK + (r & 7)
    flat = jnp.where(r < 16, r, jnp.where(r < 72, mid, (r - 64) * TOPK))
    valid = ((flat >> 4) + 1) * ((flat & 15) + 1) <= TOPK
    return flat, valid


def _route_kernel(q2_ref, keys_ref, a_ref, b_ref, g_ref, v_s, i_s, ss_s, as_s, bs_s, at_s, bt_s, gt_s, *, n_lg, ilv):
    big = jnp.int32(1 << 20)
    neg = jnp.float32(-jnp.inf)
    rows_k = lax.broadcasted_iota(I32, (N_KEYS, LANES), 0)
    rows_t = lax.broadcasted_iota(I32, (TOPK, LANES), 0)
    flat, valid = _cand_layout()
    n_grp = n_lg // ilv

    def body(it, carry):
        h = it // n_grp
        row0 = pl.multiple_of(h * TOPK, TOPK)
        for u in range(ilv):
            tok = pl.multiple_of(((it % n_grp) * ilv + u) * LANES, LANES)
            for mp in range(2):
                qt = q2_ref[2 * h + mp, pl.ds(tok, LANES), :]
                s = lax.dot_general(keys_ref[h, mp], qt, NT_DIMS, preferred_element_type=F32)
                for k in range(TOPK):
                    m = jnp.max(s, axis=0, keepdims=True)
                    idx = jnp.min(jnp.where(s == m, rows_k, big), axis=0, keepdims=True)
                    v_s[u, mp, k:k + 1, :] = m
                    i_s[u, mp, k:k + 1, :] = idx
                    s = jnp.where(rows_k == idx, neg, s)
            pieces = [v_s[u, 0, 0:1, :] + v_s[u, 1, :, :]]
            for ka in range(1, 8):
                pieces.append(v_s[u, 0, ka:ka + 1, :] + v_s[u, 1, 0:8, :])
            pieces.append(v_s[u, 0, 8:16, :] + v_s[u, 1, 0:1, :])
            cand = jnp.where(valid, jnp.concatenate(pieces, axis=0), neg)
            idx_a = i_s[u, 0]
            idx_b = i_s[u, 1]
            for k in range(TOPK):
                m = jnp.max(cand, axis=0, keepdims=True)
                p = jnp.min(jnp.where(cand == m, flat, big), axis=0, keepdims=True)
                cand = jnp.where(flat == p, neg, cand)
                ss_s[u, k:k + 1, :] = m
                as_s[u, k:k + 1, :] = jnp.max(jnp.where(rows_t == (p >> 4), idx_a, -1), axis=0, keepdims=True)
                bs_s[u, k:k + 1, :] = jnp.max(jnp.where(rows_t == (p & 15), idx_b, -1), axis=0, keepdims=True)
            ss = ss_s[u]
            e = jnp.exp(ss - ss[0:1])
            gates = e / jnp.sum(e, axis=0, keepdims=True)
            at_s[pl.ds(row0, TOPK), pl.ds(tok, LANES)] = as_s[u]
            bt_s[pl.ds(row0, TOPK), pl.ds(tok, LANES)] = bs_s[u]
            gt_s[pl.ds(row0, TOPK), pl.ds(tok, LANES)] = gates
        return carry

    lax.fori_loop(0, PEER_HEADS * n_grp, body, 0)
    for lg in range(n_lg):
        sl = slice(lg * LANES, (lg + 1) * LANES)
        a_ref[sl, :] = at_s[:, sl].T
        b_ref[sl, :] = bt_s[:, sl].T
        g_ref[sl, :] = gt_s[:, sl].T


def _route(q2, keys, tt=512, ilv=2):
    nq, t, _ = q2.shape
    n_lg = tt // LANES
    hk = PEER_HEADS * TOPK
    kern = functools.partial(_route_kernel, n_lg=n_lg, ilv=ilv)
    row = lambda i: (i, 0)
    return pl.pallas_call(
        kern,
        grid=(t // tt,),
        in_specs=[pl.BlockSpec((nq, tt, HALF_DIM), lambda i: (0, i, 0)),
                  pl.BlockSpec((PEER_HEADS, 2, N_KEYS, HALF_DIM), lambda i: (0, 0, 0, 0))],
        out_specs=[pl.BlockSpec((tt, hk), row)] * 3,
        out_shape=[jax.ShapeDtypeStruct((t, hk), I32),
                   jax.ShapeDtypeStruct((t, hk), I32),
                   jax.ShapeDtypeStruct((t, hk), F32)],
        scratch_shapes=[pltpu.VMEM((ilv, 2, TOPK, LANES), F32), pltpu.VMEM((ilv, 2, TOPK, LANES), I32),
                        pltpu.VMEM((ilv, TOPK, LANES), F32), pltpu.VMEM((ilv, TOPK, LANES), I32),
                        pltpu.VMEM((ilv, TOPK, LANES), I32),
                        pltpu.VMEM((hk, tt), I32), pltpu.VMEM((hk, tt), I32), pltpu.VMEM((hk, tt), F32)],
        compiler_params=pltpu.CompilerParams(dimension_semantics=("arbitrary",),
                                             vmem_limit_bytes=VMEM_LIMIT),
        name="route",
    )(q2, keys)


def _peer_kernel(a_ref, b_ref, g_ref, xn_ref, u_ref, v_ref, h1_ref, fg_ref, out_ref, gstage, gplanes, yacc,
                 *, ts, pitch, planes, unroll):
    c = pl.program_id(1)
    tt, d = xn_ref.shape

    @pl.when(c == 0)
    def _():
        yacc[...] = jnp.zeros_like(yacc)
        rows = lax.broadcasted_iota(I32, (N_KEYS, LANES), 0)

        def sub_tile(ti, carry):
            base = pl.multiple_of(ti * ts, ts)

            def tok(tl, carry2):
                t = base + tl
                pt = jnp.where(rows == a_ref[pl.ds(t, 1), :], g_ref[pl.ds(t, 1), :], 0.0).astype(BF16)
                qt = jnp.where(rows == b_ref[pl.ds(t, 1), :], 1.0, 0.0).astype(BF16)
                gstage[pl.ds(tl, N_KEYS, stride=pitch), :] = lax.dot_general(
                    pt, qt, NT_DIMS, preferred_element_type=F32)
                return carry2

            lax.fori_loop(0, ts, tok, 0, unroll=unroll)

            def plane(i0, carry2):
                src = pl.multiple_of(i0 * pitch, SUBLANES)
                gplanes[i0, pl.ds(base, ts), :] = gstage[pl.ds(src, ts), :].astype(BF16)
                return carry2

            lax.fori_loop(0, N_KEYS, plane, 0, unroll=4)
            return carry

        lax.fori_loop(0, tt // ts, sub_tile, 0)

    act = lax.dot_general(xn_ref[...], u_ref[...], NT_DIMS, preferred_element_type=F32)
    gate = jnp.concatenate([gplanes[c * planes + k].astype(F32) for k in range(planes)], axis=1)
    hact = (_gelu(act) * gate).astype(BF16)
    yacc[...] += jnp.dot(hact, v_ref[...], preferred_element_type=F32)

    @pl.when(c == pl.num_programs(1) - 1)
    def _():
        h2 = h1_ref[...] + yacc[...]
        out_ref[...] = (h2 * _rms_scale(h2)) * fg_ref[...]


def _peer(a, b, g, xn, u, v, h1, fg, tt=512, ts=128, planes=8, unroll=32):
    t, d = xn.shape
    ne = u.shape[0]
    ec = planes * N_KEYS
    pitch = ts + SUBLANES
    kern = functools.partial(_peer_kernel, ts=ts, pitch=pitch, planes=planes, unroll=unroll)
    row = lambda i, c: (i, 0)
    table = lambda i, c: (c, 0)
    return pl.pallas_call(
        kern,
        grid=(t // tt, ne // ec),
        in_specs=[pl.BlockSpec((tt, LANES), row), pl.BlockSpec((tt, LANES), row), pl.BlockSpec((tt, LANES), row),
                  pl.BlockSpec((tt, d), row),
                  pl.BlockSpec((ec, d), table),
                  pl.BlockSpec((ec, d), table),
                  pl.BlockSpec((tt, d), row),
                  pl.BlockSpec((1, d), lambda i, c: (0, 0))],
        out_specs=pl.BlockSpec((tt, d), row),
        out_shape=jax.ShapeDtypeStruct((t, d), F32),
        scratch_shapes=[pltpu.VMEM((N_KEYS * pitch, LANES), F32),
                        pltpu.VMEM((N_KEYS, tt, LANES), BF16),
                        pltpu.VMEM((tt, d), F32)],
        compiler_params=pltpu.CompilerParams(dimension_semantics=("arbitrary", "arbitrary"),
                                             vmem_limit_bytes=VMEM_LIMIT),
        name="peer",
    )(a, b, g, xn, u, v, h1, fg)


def kernel(x, norm1_g, w_in, lambda_qk, subln_g, conv_w, w_attn_o, w_conv_o, w_out, norm2_g, w_query,
           sub_keys, expert_u, expert_v, final_g):
    batch, seq, d = x.shape
    assert w_in.shape[0] == 1, "the final RMSNorm is fused into the single layer's PEER kernel"
    l = 0
    h = x.reshape(batch * seq, d)
    lam_init = 0.8 - 0.6 * math.exp(-0.3 * l)
    qkv, z, sa, sc = _in_proj(h, norm1_g[l][None], w_in[l].astype(BF16), conv_w[l], seq)
    o = _diff_attn(qkv, lambda_qk[l], subln_g[l][None], batch, seq, lam_init)
    h1, xn, q2 = _mix_out(o, z, sa, sc, h, w_attn_o[l].astype(BF16), w_conv_o[l].astype(BF16),
                          w_out[l].astype(BF16), norm2_g[l][None], w_query[l].astype(BF16))
    a, b, g = _route(q2, sub_keys[l].astype(BF16))
    out = _peer(a, b, g, xn, expert_u[l].astype(BF16), expert_v[l].astype(BF16), h1, final_g[None])
    return out.reshape(batch, seq, d)
```

```python
import functools
import math

import jax
import jax.numpy as jnp
from jax import lax
from jax.experimental import pallas as pl
from jax.experimental.pallas import tpu as pltpu

F32 = jnp.float32
BF16 = jnp.bfloat16
I32 = jnp.int32

RMS_EPS = 1e-6
N_HEADS = 8
HEAD_DIM = 64
V_DIM = 128
PEER_HEADS = 8
N_KEYS = 128
HALF_DIM = 128
TOPK = 16
LANES = 128
SUBLANES = 8
VMEM_LIMIT = 56 * 1024 * 1024
PEER_TILE = 512
ROUTE_ILV = 2

NT_DIMS = (((1,), (1,)), ((), ()))


def _gelu(x):
    return 0.5 * x * (1.0 + lax.erf(x * (2.0 ** -0.5)))


def _rms_scale(x):
    return lax.rsqrt(jnp.mean(x * x, axis=-1, keepdims=True) + RMS_EPS)


def _inproj_kernel(x_ref, g1_ref, w_ref, cw_ref, qkv_ref, z_ref, sa_ref, sc_ref, n_ref, carry_ref,
                   *, tiles_per_seq, d, cb):
    i = pl.program_id(0)
    tt = x_ref.shape[0]
    x = x_ref[...]
    n_ref[...] = ((x * _rms_scale(x)) * g1_ref[...]).astype(BF16)

    @pl.when(i % tiles_per_seq == 0)
    def _():
        carry_ref[...] = jnp.zeros_like(carry_ref)

    def proj(col):
        return jnp.dot(n_ref[...], w_ref[:, col:col + cb], preferred_element_type=F32)

    for j in range(3 * d // cb):
        qkv_ref[:, j * cb:(j + 1) * cb] = proj(j * cb).astype(BF16)
    for j in range(d // cb):
        c0 = j * cb
        gate_b = proj(3 * d + c0)
        u = proj(4 * d + c0) * proj(5 * d + c0)
        ext = jnp.concatenate([carry_ref[:, c0:c0 + cb], u], axis=0)
        conv = (cw_ref[0:1, c0:c0 + cb] * ext[SUBLANES - 2:SUBLANES - 2 + tt]
                + cw_ref[1:2, c0:c0 + cb] * ext[SUBLANES - 1:SUBLANES - 1 + tt]
                + cw_ref[2:3, c0:c0 + cb] * u)
        z_ref[:, c0:c0 + cb] = (gate_b * conv).astype(BF16)
        carry_ref[:, c0:c0 + cb] = u[tt - SUBLANES:, :]
    for j in range(d // cb):
        c0 = j * cb
        sa_ref[:, c0:c0 + cb] = jax.nn.sigmoid(proj(6 * d + c0))
        sc_ref[:, c0:c0 + cb] = jax.nn.sigmoid(proj(7 * d + c0))


def _in_proj(x2, g1, w_in, conv_w, seq, tt=256, cb=512):
    t, d = x2.shape
    kern = functools.partial(_inproj_kernel, tiles_per_seq=seq // tt, d=d, cb=cb)
    row = lambda i: (i, 0)
    fixed = lambda i: (0, 0)
    return pl.pallas_call(
        kern,
        grid=(t // tt,),
        in_specs=[pl.BlockSpec((tt, d), row),
                  pl.BlockSpec((1, d), fixed),
                  pl.BlockSpec((d, 8 * d), fixed, pipeline_mode=pl.Buffered(1)),
                  pl.BlockSpec((3, d), fixed)],
        out_specs=[pl.BlockSpec((tt, 3 * d), row),
                   pl.BlockSpec((tt, d), row),
                   pl.BlockSpec((tt, d), row),
                   pl.BlockSpec((tt, d), row)],
        out_shape=[jax.ShapeDtypeStruct((t, 3 * d), BF16),
                   jax.ShapeDtypeStruct((t, d), BF16),
                   jax.ShapeDtypeStruct((t, d), F32),
                   jax.ShapeDtypeStruct((t, d), F32)],
        scratch_shapes=[pltpu.VMEM((tt, d), BF16), pltpu.VMEM((SUBLANES, d), F32)],
        compiler_params=pltpu.CompilerParams(dimension_semantics=("arbitrary",),
                                             vmem_limit_bytes=VMEM_LIMIT),
        name="in_proj",
    )(x2, g1, w_in, conv_w)


def _attn_kernel(q_ref, k_ref, v_ref, lam_ref, sg_ref, o_ref, s_sc, mrun_sc, lrun_sc, acc_sc,
                 *, lam_init, bq, hps):
    i = pl.program_id(2)
    lq = lam_ref[...]
    lam = (jnp.exp(jnp.sum(lq[0:1] * lq[1:2], axis=-1, keepdims=True))
           - jnp.exp(jnp.sum(lq[2:3] * lq[3:4], axis=-1, keepdims=True)) + lam_init)
    nt = bq // LANES
    heads = [slice(hh * V_DIM, (hh + 1) * V_DIM) for hh in range(hps)]

    q2 = []
    for cols in heads:
        q = q_ref[:, cols] * (HEAD_DIM ** -0.5)
        lane = lax.broadcasted_iota(I32, q.shape, 1)
        zero = jnp.zeros_like(q)
        q2.append(jnp.concatenate([jnp.where(lane < HEAD_DIM, q, zero), jnp.where(lane >= HEAD_DIM, q, zero)],
                                  axis=0))

    def scores(hh, c):
        off = pl.multiple_of(c * bq, bq)
        return lax.dot_general(q2[hh], k_ref[pl.ds(off, bq), heads[hh]], NT_DIMS,
                               preferred_element_type=F32)

    def tile_max(s):
        return functools.reduce(jnp.maximum, [s[:, j * LANES:(j + 1) * LANES] for j in range(nt)])

    mrun_sc[...] = jnp.full(mrun_sc.shape, -jnp.inf, F32)

    def pass1(c, carry):
        for hh in range(hps):
            s = scores(hh, c)
            s_sc[hh, c] = s
            mrun_sc[hh] = jnp.maximum(mrun_sc[hh], tile_max(s))
        return carry

    lax.fori_loop(0, i, pass1, 0)
    for hh in range(hps):
        s = scores(hh, i)
        r = lax.broadcasted_iota(I32, s.shape, 0) & (bq - 1)
        cc = lax.broadcasted_iota(I32, s.shape, 1)
        s = jnp.where(cc <= r, s, -jnp.inf)
        s_sc[hh, i] = s
        mrow = jnp.max(jnp.maximum(mrun_sc[hh], tile_max(s)), axis=-1, keepdims=True)
        mrun_sc[hh] = jnp.broadcast_to(mrow, mrun_sc.shape[1:])

    lrun_sc[...] = jnp.zeros_like(lrun_sc)
    acc_sc[...] = jnp.zeros_like(acc_sc)

    def pass2(c, carry):
        off = pl.multiple_of(c * bq, bq)
        for hh in range(hps):
            mb = mrun_sc[hh]
            s = s_sc[hh, c]
            ps = [jnp.exp(s[:, j * LANES:(j + 1) * LANES] - mb) for j in range(nt)]
            lrun_sc[hh] += functools.reduce(jnp.add, ps)
            p = jnp.concatenate(ps, axis=1).astype(BF16)
            acc_sc[hh] += jnp.dot(p, v_ref[pl.ds(off, bq), heads[hh]], preferred_element_type=F32)
        return carry

    lax.fori_loop(0, i + 1, pass2, 0)
    for hh in range(hps):
        den = jnp.sum(lrun_sc[hh], axis=-1, keepdims=True)
        on = acc_sc[hh] / den
        o = on[:bq] - lam * on[bq:]
        o = (o * _rms_scale(o)) * sg_ref[...] * (1.0 - lam_init)
        o_ref[:, heads[hh]] = o.astype(BF16)


def _diff_attn(qkv, lambda_qk, subln_g, batch, seq, lam_init, bq=256, hps=4):
    t = qkv.shape[0]
    nq = seq // bq
    hb = N_HEADS // hps
    wb = hps * V_DIM
    assert bq & (bq - 1) == 0 and bq % LANES == 0
    kern = functools.partial(_attn_kernel, lam_init=lam_init, bq=bq, hps=hps)
    return pl.pallas_call(
        kern,
        grid=(batch, hb, nq),
        in_specs=[pl.BlockSpec((bq, wb), lambda b, h, i: (b * nq + i, h)),
                  pl.BlockSpec((seq, wb), lambda b, h, i: (b, hb + h)),
                  pl.BlockSpec((seq, wb), lambda b, h, i: (b, 2 * hb + h)),
                  pl.BlockSpec((4, HEAD_DIM), lambda b, h, i: (0, 0)),
                  pl.BlockSpec((1, V_DIM), lambda b, h, i: (0, 0))],
        out_specs=pl.BlockSpec((bq, wb), lambda b, h, i: (b * nq + i, h)),
        out_shape=jax.ShapeDtypeStruct((t, N_HEADS * V_DIM), BF16),
        scratch_shapes=[pltpu.VMEM((hps, nq, 2 * bq, bq), F32), pltpu.VMEM((hps, 2 * bq, LANES), F32),
                        pltpu.VMEM((hps, 2 * bq, LANES), F32), pltpu.VMEM((hps, 2 * bq, V_DIM), F32)],
        compiler_params=pltpu.CompilerParams(dimension_semantics=("arbitrary",) * 3,
                                             vmem_limit_bytes=VMEM_LIMIT),
        name="diff_attn",
    )(qkv, qkv, qkv, lambda_qk, subln_g)


def _mixout_kernel(o_ref, z_ref, sa_ref, sc_ref, x_ref, wao_ref, wco_ref, wout_ref, g2_ref, wq_ref,
                   h1_ref, xn_ref, q2_ref):
    ya = jnp.dot(o_ref[...], wao_ref[...], preferred_element_type=F32)
    yc = jnp.dot(z_ref[...], wco_ref[...], preferred_element_type=F32)
    merged = sa_ref[...] * ya + sc_ref[...] * yc
    h1 = x_ref[...] + jnp.dot(merged.astype(BF16), wout_ref[...], preferred_element_type=F32)
    h1_ref[...] = h1
    xn = ((h1 * _rms_scale(h1)) * g2_ref[...]).astype(BF16)
    xn_ref[...] = xn
    q2 = jnp.dot(xn, wq_ref[...], preferred_element_type=F32).astype(BF16)
    for j in range(q2_ref.shape[0]):
        q2_ref[j] = q2[:, j * HALF_DIM:(j + 1) * HALF_DIM]


def _mix_out(o, z, sa, sc, x2, wao, wco, wout, g2, wq, tt=256):
    t, d = x2.shape
    nq = wq.shape[1] // HALF_DIM
    row = lambda i: (i, 0)
    fixed = lambda i: (0, 0)
    return pl.pallas_call(
        _mixout_kernel,
        grid=(t // tt,),
        in_specs=[pl.BlockSpec((tt, d), row), pl.BlockSpec((tt, d), row),
                  pl.BlockSpec((tt, d), row), pl.BlockSpec((tt, d), row),
                  pl.BlockSpec((tt, d), row),
                  pl.BlockSpec((d, d), fixed), pl.BlockSpec((d, d), fixed), pl.BlockSpec((d, d), fixed),
                  pl.BlockSpec((1, d), fixed),
                  pl.BlockSpec((d, nq * HALF_DIM), fixed)],
        out_specs=[pl.BlockSpec((tt, d), row), pl.BlockSpec((tt, d), row),
                   pl.BlockSpec((nq, tt, HALF_DIM), lambda i: (0, i, 0))],
        out_shape=[jax.ShapeDtypeStruct((t, d), F32),
                   jax.ShapeDtypeStruct((t, d), BF16),
                   jax.ShapeDtypeStruct((nq, t, HALF_DIM), BF16)],
        compiler_params=pltpu.CompilerParams(dimension_semantics=("arbitrary",),
                                             vmem_limit_bytes=VMEM_LIMIT),
        name="mix_out",
    )(o, z, sa, sc, x2, wao, wco, wout, g2, wq)


def _cand_layout():
    r = lax.broadcasted_iota(I32, (80, LANES), 0)
    mid = ((r - 8) >> 3) * TOPK + (r & 7)
    flat = jnp.where(r < 16, r, jnp.where(r < 72, mid, (r - 64) * TOPK))
    valid = ((flat >> 4) + 1) * ((flat & 15) + 1) <= TOPK
    return flat, valid


def _route_scratch(tt, ilv):
    hk = PEER_HEADS * TOPK
    return [pltpu.VMEM((ilv, 2, TOPK, LANES), F32), pltpu.VMEM((ilv, 2, TOPK, LANES), I32),
            pltpu.VMEM((ilv, TOPK, LANES), F32), pltpu.VMEM((ilv, TOPK, LANES), I32),
            pltpu.VMEM((ilv, TOPK, LANES), I32),
            pltpu.VMEM((hk, tt), I32), pltpu.VMEM((hk, tt), I32), pltpu.VMEM((hk, tt), F32)]


def _route_body(q2_ref, keys_ref, it, sc, *, n_grp, ilv):
    v_s, i_s, ss_s, as_s, bs_s, at_s, bt_s, gt_s = sc
    big = jnp.int32(1 << 20)
    neg = jnp.float32(-jnp.inf)
    rows_k = lax.broadcasted_iota(I32, (N_KEYS, LANES), 0)
    rows_t = lax.broadcasted_iota(I32, (TOPK, LANES), 0)
    flat, valid = _cand_layout()
    h = it // n_grp
    row0 = pl.multiple_of(h * TOPK, TOPK)
    for u in range(ilv):
        tok = pl.multiple_of(((it % n_grp) * ilv + u) * LANES, LANES)
        for mp in range(2):
            qt = q2_ref[2 * h + mp, pl.ds(tok, LANES), :]
            s = lax.dot_general(keys_ref[h, mp], qt, NT_DIMS, preferred_element_type=F32)
            for k in range(TOPK):
                m = jnp.max(s, axis=0, keepdims=True)
                idx = jnp.min(jnp.where(s == m, rows_k, big), axis=0, keepdims=True)
                v_s[u, mp, k:k + 1, :] = m
                i_s[u, mp, k:k + 1, :] = idx
                s = jnp.where(rows_k == idx, neg, s)
        pieces = [v_s[u, 0, 0:1, :] + v_s[u, 1, :, :]]
        for ka in range(1, 8):
            pieces.append(v_s[u, 0, ka:ka + 1, :] + v_s[u, 1, 0:8, :])
        pieces.append(v_s[u, 0, 8:16, :] + v_s[u, 1, 0:1, :])
        cand = jnp.where(valid, jnp.concatenate(pieces, axis=0), neg)
        idx_a = i_s[u, 0]
        idx_b = i_s[u, 1]
        for k in range(TOPK):
            m = jnp.max(cand, axis=0, keepdims=True)
            p = jnp.min(jnp.where(cand == m, flat, big), axis=0, keepdims=True)
            cand = jnp.where(flat == p, neg, cand)
            ss_s[u, k:k + 1, :] = m
            as_s[u, k:k + 1, :] = jnp.max(jnp.where(rows_t == (p >> 4), idx_a, -1), axis=0, keepdims=True)
            bs_s[u, k:k + 1, :] = jnp.max(jnp.where(rows_t == (p & 15), idx_b, -1), axis=0, keepdims=True)
        ss = ss_s[u]
        e = jnp.exp(ss - ss[0:1])
        gates = e / jnp.sum(e, axis=0, keepdims=True)
        at_s[pl.ds(row0, TOPK), pl.ds(tok, LANES)] = as_s[u]
        bt_s[pl.ds(row0, TOPK), pl.ds(tok, LANES)] = bs_s[u]
        gt_s[pl.ds(row0, TOPK), pl.ds(tok, LANES)] = gates


def _route_finish(sc, a_dst, b_dst, g_dst):
    at_s, bt_s, gt_s = sc[5:]
    for lg in range(at_s.shape[1] // LANES):
        sl = slice(lg * LANES, (lg + 1) * LANES)
        a_dst[sl, :] = at_s[:, sl].T
        b_dst[sl, :] = bt_s[:, sl].T
        g_dst[sl, :] = gt_s[:, sl].T


def _route_kernel(q2_ref, keys_ref, a_ref, b_ref, g_ref, *sc, n_grp, ilv):
    def body(it, carry):
        _route_body(q2_ref, keys_ref, it, sc, n_grp=n_grp, ilv=ilv)
        return carry

    lax.fori_loop(0, PEER_HEADS * n_grp, body, 0)
    _route_finish(sc, a_ref, b_ref, g_ref)


def _route(q2, keys, n_tiles, tt, ilv):
    nq = q2.shape[0]
    hk = PEER_HEADS * TOPK
    kern = functools.partial(_route_kernel, n_grp=tt // LANES // ilv, ilv=ilv)
    row = lambda i: (i, 0)
    return pl.pallas_call(
        kern,
        grid=(n_tiles,),
        in_specs=[pl.BlockSpec((nq, tt, HALF_DIM), lambda i: (0, i, 0)),
                  pl.BlockSpec((PEER_HEADS, 2, N_KEYS, HALF_DIM), lambda i: (0, 0, 0, 0))],
        out_specs=[pl.BlockSpec((tt, hk), row)] * 3,
        out_shape=[jax.ShapeDtypeStruct((n_tiles * tt, hk), I32),
                   jax.ShapeDtypeStruct((n_tiles * tt, hk), I32),
                   jax.ShapeDtypeStruct((n_tiles * tt, hk), F32)],
        scratch_shapes=_route_scratch(tt, ilv),
        compiler_params=pltpu.CompilerParams(dimension_semantics=("arbitrary",),
                                             vmem_limit_bytes=VMEM_LIMIT),
        name="route",
    )(q2, keys)


def _peer_kernel(a0_ref, b0_ref, g0_ref, q2n_ref, keys_ref, xn_ref, u_ref, v_ref, h1_ref, fg_ref, out_ref,
                 gstage, gplanes, yacc, a_sc, b_sc, g_sc, *route_sc, ts, pitch, planes, unroll, n_grp, ilv):
    i = pl.program_id(0)
    c = pl.program_id(1)
    tt, d = xn_ref.shape

    @pl.when(c == 0)
    def _():
        @pl.when(i == 0)
        def _():
            a_sc[...] = a0_ref[...]
            b_sc[...] = b0_ref[...]
            g_sc[...] = g0_ref[...]

        yacc[...] = jnp.zeros_like(yacc)
        rows = lax.broadcasted_iota(I32, (N_KEYS, LANES), 0)

        def sub_tile(ti, carry):
            base = pl.multiple_of(ti * ts, ts)

            def tok(tl, carry2):
                t = base + tl
                pt = jnp.where(rows == a_sc[pl.ds(t, 1), :], g_sc[pl.ds(t, 1), :], 0.0).astype(BF16)
                qt = jnp.where(rows == b_sc[pl.ds(t, 1), :], 1.0, 0.0).astype(BF16)
                gstage[pl.ds(tl, N_KEYS, stride=pitch), :] = lax.dot_general(
                    pt, qt, NT_DIMS, preferred_element_type=F32)
                return carry2

            lax.fori_loop(0, ts, tok, 0, unroll=unroll)

            def plane(i0, carry2):
                src = pl.multiple_of(i0 * pitch, SUBLANES)
                gplanes[i0, pl.ds(base, ts), :] = gstage[pl.ds(src, ts), :].astype(BF16)
                return carry2

            lax.fori_loop(0, N_KEYS, plane, 0, unroll=4)
            return carry

        lax.fori_loop(0, tt // ts, sub_tile, 0)

    _route_body(q2n_ref, keys_ref, c, route_sc, n_grp=n_grp, ilv=ilv)
    act = lax.dot_general(xn_ref[...], u_ref[...], NT_DIMS, preferred_element_type=F32)
    gate = jnp.concatenate([gplanes[c * planes + k].astype(F32) for k in range(planes)], axis=1)
    hact = (_gelu(act) * gate).astype(BF16)
    yacc[...] += jnp.dot(hact, v_ref[...], preferred_element_type=F32)

    @pl.when(c == pl.num_programs(1) - 1)
    def _():
        h2 = h1_ref[...] + yacc[...]
        out_ref[...] = (h2 * _rms_scale(h2)) * fg_ref[...]
        _route_finish(route_sc, a_sc, b_sc, g_sc)


def _peer(a0, b0, g0, q2, keys, xn, u, v, h1, fg, tt, ilv, ts=64, planes=8, unroll=64):
    t, d = xn.shape
    ne = u.shape[0]
    nq = q2.shape[0]
    ec = planes * N_KEYS
    n_tiles = t // tt
    n_grp = tt // LANES // ilv
    assert ne // ec == PEER_HEADS * n_grp, "one routing body per expert step"
    pitch = ts + SUBLANES
    kern = functools.partial(_peer_kernel, ts=ts, pitch=pitch, planes=planes, unroll=unroll, n_grp=n_grp, ilv=ilv)
    row = lambda i, c: (i, 0)
    first = lambda i, c: (0, 0)
    table = lambda i, c: (c, 0)
    hk = PEER_HEADS * TOPK
    return pl.pallas_call(
        kern,
        grid=(n_tiles, ne // ec),
        in_specs=[pl.BlockSpec((tt, hk), first), pl.BlockSpec((tt, hk), first), pl.BlockSpec((tt, hk), first),
                  pl.BlockSpec((nq, tt, HALF_DIM), lambda i, c: (0, jnp.minimum(i + 1, n_tiles - 1), 0)),
                  pl.BlockSpec((PEER_HEADS, 2, N_KEYS, HALF_DIM), lambda i, c: (0, 0, 0, 0)),
                  pl.BlockSpec((tt, d), row),
                  pl.BlockSpec((ec, d), table),
                  pl.BlockSpec((ec, d), table),
                  pl.BlockSpec((tt, d), row),
                  pl.BlockSpec((1, d), first)],
        out_specs=pl.BlockSpec((tt, d), row),
        out_shape=jax.ShapeDtypeStruct((t, d), F32),
        scratch_shapes=[pltpu.VMEM((N_KEYS * pitch, LANES), F32),
                        pltpu.VMEM((N_KEYS, tt, LANES), BF16),
                        pltpu.VMEM((tt, d), F32),
                        pltpu.VMEM((tt, hk), I32), pltpu.VMEM((tt, hk), I32), pltpu.VMEM((tt, hk), F32)]
                       + _route_scratch(tt, ilv),
        compiler_params=pltpu.CompilerParams(dimension_semantics=("arbitrary", "arbitrary"),
                                             vmem_limit_bytes=VMEM_LIMIT),
        name="peer",
    )(a0, b0, g0, q2, keys, xn, u, v, h1, fg)


def kernel(x, norm1_g, w_in, lambda_qk, subln_g, conv_w, w_attn_o, w_conv_o, w_out, norm2_g, w_query,
           sub_keys, expert_u, expert_v, final_g):
    batch, seq, d = x.shape
    assert w_in.shape[0] == 1, "the final RMSNorm is fused into the single layer's PEER kernel"
    l = 0
    h = x.reshape(batch * seq, d)
    lam_init = 0.8 - 0.6 * math.exp(-0.3 * l)
    qkv, z, sa, sc = _in_proj(h, norm1_g[l][None], w_in[l].astype(BF16), conv_w[l], seq)
    o = _diff_attn(qkv, lambda_qk[l], subln_g[l][None], batch, seq, lam_init)
    h1, xn, q2 = _mix_out(o, z, sa, sc, h, w_attn_o[l].astype(BF16), w_conv_o[l].astype(BF16),
                          w_out[l].astype(BF16), norm2_g[l][None], w_query[l].astype(BF16))
    keys = sub_keys[l].astype(BF16)
    a0, b0, g0 = _route(q2, keys, n_tiles=1, tt=PEER_TILE, ilv=ROUTE_ILV)
    out = _peer(a0, b0, g0, q2, keys, xn, expert_u[l].astype(BF16), expert_v[l].astype(BF16), h1, final_g[None],
                tt=PEER_TILE, ilv=ROUTE_ILV)
    return out.reshape(batch, seq, d)
```
